```python
import jax, jax.numpy as jnp
from jax import lax
import numpy as np

D_MODEL = 2048
BATCH = 8
SEQ = 2048
DEPTH = 2

GRID_W = 64
CTX_LEN = 256
HEAD_DIM = 128
NA_HEADS = 8
NA_WIN_H = 8
NA_WIN_W = 16
GQA_HEADS = 8
GQA_KV_HEADS = 2
MLA_HEADS = 8
MLA_Q_RANK = 512
MLA_KV_RANK = 512
MLA_NOPE = 128
MLA_ROPE = 64
MLA_V = 128
A_W = NA_HEADS * HEAD_DIM
B_QW = GQA_HEADS * HEAD_DIM
B_KVW = GQA_KV_HEADS * HEAD_DIM
C_W = MLA_HEADS * MLA_V
D_FF = -(-8 * D_MODEL // (3 * 256)) * 256
IN_SIZES = (A_W, A_W, A_W, B_QW, B_KVW, B_KVW, MLA_Q_RANK, MLA_KV_RANK, MLA_ROPE, D_MODEL, D_MODEL, D_MODEL)
IN_WIDTH = sum(IN_SIZES)
ROPE_THETA = 10000.0
Q_BLOCK = 128
EPS = 1e-6
NEG_INF = -1e30

kernel_name = 'hybrid_natten_gqa_mla_prefix_dit'


def _rms_norm(x, g):
    xf = x.astype(jnp.float32)
    y = xf * lax.rsqrt(jnp.mean(xf * xf, axis=-1, keepdims=True) + EPS)
    return (y * g.astype(jnp.float32)).astype(x.dtype)


def _rope_1d(x, pos):
    d = x.shape[-1]
    freqs = ROPE_THETA ** (-jnp.arange(0, d, 2, dtype=jnp.float32) / d)
    ang = pos.astype(jnp.float32)[:, None] * freqs[None, :]
    cos, sin = jnp.cos(ang), jnp.sin(ang)
    xf = x.astype(jnp.float32)
    x1, x2 = xf[..., : d // 2], xf[..., d // 2:]
    return jnp.concatenate([x1 * cos - x2 * sin, x2 * cos + x1 * sin], axis=-1).astype(x.dtype)


def _rope_2d(x, row, col):
    half = x.shape[-1] // 2
    return jnp.concatenate([_rope_1d(x[..., :half], row), _rope_1d(x[..., half:], col)], axis=-1)


def _heads(z, n):
    b, t, _ = z.shape
    return z.reshape(b, t, n, -1).transpose(0, 2, 1, 3)


def _merge_heads(o):
    b, h, t, d = o.shape
    return o.transpose(0, 2, 1, 3).reshape(b, t, h * d)


def _project_in(h, w_in):
    z = jnp.einsum('btd,de->bte', h, w_in)
    return jnp.split(z, np.cumsum(IN_SIZES)[:-1].tolist(), axis=-1)


def _attend(q, k, v, scale):
    s = jnp.einsum('bhgqd,bhkd->bhgqk', q, k, preferred_element_type=jnp.float32) * scale
    p = jax.nn.softmax(s, axis=-1).astype(v.dtype)
    return jnp.einsum('bhgqk,bhkd->bhgqd', p, v)


def _blocked_attention(q, k, v, k_ctx, v_ctx, scale):
    b, hk, g, s, d = q.shape
    nb = s // Q_BLOCK
    k_all = jnp.concatenate([k, k_ctx], axis=2)
    v_all = jnp.concatenate([v, v_ctx], axis=2)
    qb = jnp.moveaxis(q.reshape(b, hk, g, nb, Q_BLOCK, d), 3, 0)
    o = lax.map(lambda qi: _attend(qi, k_all, v_all, scale), qb)
    return jnp.moveaxis(o, 0, 3).reshape(b, hk, g, s, v.shape[-1])


def _neighborhood_attention(q, k, v, k_ctx, v_ctx, rpb):
    b, h, s, d = q.shape
    rows = s // GRID_W
    kh, kw = min(NA_WIN_H, rows), NA_WIN_W
    scale = d ** -0.5
    qg = q.reshape(b, h, rows, GRID_W, d)
    kg = k.reshape(b, h, rows, GRID_W, d)
    vg = v.reshape(b, h, rows, GRID_W, d)
    qc = np.arange(GRID_W)
    c0 = np.clip(qc - kw // 2, 0, GRID_W - kw)
    in_win = (qc[None, :] >= c0[:, None]) & (qc[None, :] < c0[:, None] + kw)
    col_mask = jnp.where(jnp.asarray(in_win), 0.0, NEG_INF)[:, None, :]
    dc_idx = np.clip(qc[None, :] - qc[:, None], -(kw - 1), kw - 1) + (NA_WIN_W - 1)

    def row_block(r):
        r0 = jnp.clip(r - kh // 2, 0, rows - kh)
        q_r = lax.dynamic_index_in_dim(qg, r, axis=2, keepdims=False)
        k_b = lax.dynamic_slice_in_dim(kg, r0, kh, axis=2)
        v_b = lax.dynamic_slice_in_dim(vg, r0, kh, axis=2)
        dr_idx = r0 + jnp.arange(kh) - r + (NA_WIN_H - 1)
        bias = rpb[:, dr_idx[None, :, None], dc_idx[:, None, :]]
        s_win = jnp.einsum('bhqd,bhiwd->bhqiw', q_r, k_b, preferred_element_type=jnp.float32) * scale + bias + col_mask
        s_ctx = jnp.einsum('bhqd,bhcd->bhqc', q_r, k_ctx, preferred_element_type=jnp.float32) * scale
        s_all = jnp.concatenate([s_win.reshape(b, h, GRID_W, kh * GRID_W), s_ctx], axis=-1)
        p = jax.nn.softmax(s_all, axis=-1).astype(v.dtype)
        p_win = p[..., : kh * GRID_W].reshape(b, h, GRID_W, kh, GRID_W)
        p_ctx = p[..., kh * GRID_W:]
        return (jnp.einsum('bhqiw,bhiwd->bhqd', p_win, v_b)
                + jnp.einsum('bhqc,bhcd->bhqd', p_ctx, v_ctx))

    o = lax.map(row_block, jnp.arange(rows))
    return jnp.moveaxis(o, 0, 2).reshape(b, h, s, d)


def _gqa_qkv(bq, bk, bv, q_norm, k_norm, row, col):
    q = _rms_norm(_heads(bq, GQA_HEADS), q_norm)
    k = _rms_norm(_heads(bk, GQA_KV_HEADS), k_norm)
    v = _heads(bv, GQA_KV_HEADS)
    if row is not None:
        q, k = _rope_2d(q, row, col), _rope_2d(k, row, col)
    b, hq, t, d = q.shape
    return q.reshape(b, GQA_KV_HEADS, hq // GQA_KV_HEADS, t, d), k, v


def _mla_qkv(cq, ckv, ckr, q_norm, kv_norm, w_uq, w_ukv, row, col):
    q = _heads(jnp.einsum('btr,re->bte', _rms_norm(cq, q_norm), w_uq), MLA_HEADS)
    kv = _heads(jnp.einsum('btr,re->bte', _rms_norm(ckv, kv_norm), w_ukv), MLA_HEADS)
    q_nope, q_rope = q[..., :MLA_NOPE], q[..., MLA_NOPE:]
    k_nope, v = kv[..., :MLA_NOPE], kv[..., MLA_NOPE:]
    k_rope = ckr[:, None]
    if row is not None:
        q_rope, k_rope = _rope_2d(q_rope, row, col), _rope_2d(k_rope, row, col)
    k_rope = jnp.broadcast_to(k_rope, k_nope.shape[:-1] + (MLA_ROPE,))
    q = jnp.concatenate([q_nope, q_rope], axis=-1)[:, :, None]
    k = jnp.concatenate([k_nope, k_rope], axis=-1)
    return q, k, v


def _merge(o_a, o_b, o_c, ga, gb, gc, w_br_a, w_br_b, w_br_c, w_o):
    def branch(o, w):
        return jnp.einsum('bte,ed->btd', _merge_heads(o), w)
    y = (jax.nn.sigmoid(ga) * branch(o_a, w_br_a)
         + jax.nn.sigmoid(gb) * branch(o_b, w_br_b)
         + jax.nn.sigmoid(gc) * branch(o_c, w_br_c))
    return jnp.einsum('btd,de->bte', y, w_o)


def _token_mixer(h, hc, w_in, rpb, gqa_qn, gqa_kn, mla_qn, mla_kvn, w_uq, w_ukv,
                 w_br_a, w_br_b, w_br_c, w_o, row, col, ctx_out):
    aq, ak, av, bq, bk, bv, cq, ckv, ckr, ga, gb, gc = _project_in(h, w_in)
    aq_c, ak_c, av_c, bq_c, bk_c, bv_c, cq_c, ckv_c, ckr_c, ga_c, gb_c, gc_c = _project_in(hc, w_in)
    ka_c, va_c = _heads(ak_c, NA_HEADS), _heads(av_c, NA_HEADS)
    o_a = _neighborhood_attention(_heads(aq, NA_HEADS), _heads(ak, NA_HEADS), _heads(av, NA_HEADS), ka_c, va_c, rpb)
    qb, kb, vb = _gqa_qkv(bq, bk, bv, gqa_qn, gqa_kn, row, col)
    qb_c, kb_c, vb_c = _gqa_qkv(bq_c, bk_c, bv_c, gqa_qn, gqa_kn, None, None)
    scale_b = HEAD_DIM ** -0.5
    o_b = _blocked_attention(qb, kb, vb, kb_c, vb_c, scale_b)
    o_b = o_b.reshape(o_b.shape[0], GQA_HEADS, o_b.shape[3], HEAD_DIM)
    qc, kc, vc = _mla_qkv(cq, ckv, ckr, mla_qn, mla_kvn, w_uq, w_ukv, row, col)
    qc_c, kc_c, vc_c = _mla_qkv(cq_c, ckv_c, ckr_c, mla_qn, mla_kvn, w_uq, w_ukv, None, None)
    scale_c = (MLA_NOPE + MLA_ROPE) ** -0.5
    o_c = _blocked_attention(qc, kc, vc, kc_c, vc_c, scale_c)[:, :, 0]
    y = _merge(o_a, o_b, o_c, ga, gb, gc, w_br_a, w_br_b, w_br_c, w_o)
    if not ctx_out:
        return y, None
    o_a_c = _attend(_heads(aq_c, NA_HEADS)[:, :, None], ka_c, va_c, HEAD_DIM ** -0.5)[:, :, 0]
    o_b_c = _attend(qb_c, kb_c, vb_c, scale_b)
    o_b_c = o_b_c.reshape(o_b_c.shape[0], GQA_HEADS, o_b_c.shape[3], HEAD_DIM)
    o_c_c = _attend(qc_c, kc_c, vc_c, scale_c)[:, :, 0]
    yc = _merge(o_a_c, o_b_c, o_c_c, ga_c, gb_c, gc_c, w_br_a, w_br_b, w_br_c, w_o)
    return y, yc


def _swiglu(h, w1, w3, w2):
    a = jnp.einsum('btd,df->btf', h, w1)
    g = jnp.einsum('btd,df->btf', h, w3)
    return jnp.einsum('btf,fd->btd', jax.nn.silu(a) * g, w2)


def setup_inputs(seed: int = 0) -> dict:
    key = jax.random.key(seed)
    ks = jax.random.split(key, 25)
    L, D = DEPTH, D_MODEL

    def nrm(k, shape, std):
        return jax.random.normal(k, shape, jnp.float32) * std

    def gain(k, shape):
        return 1.0 + 0.1 * jax.random.normal(k, shape, jnp.float32)

    return {
        'x': nrm(ks[0], (BATCH, SEQ, D), 1.0),
        'c': nrm(ks[1], (BATCH, D), 1.0),
        'ctx': nrm(ks[2], (BATCH, CTX_LEN, D), 1.0),
        'c_ctx': nrm(ks[3], (D,), 1.0),
        'w_ada': nrm(ks[4], (L, D, 6 * D), 0.5 * D ** -0.5),
        'b_ada': nrm(ks[5], (L, 6 * D), 0.01),
        'g_pre1': gain(ks[6], (L, D)),
        'g_post1': gain(ks[7], (L, D)),
        'g_pre2': gain(ks[8], (L, D)),
        'g_post2': gain(ks[9], (L, D)),
        'w_in': nrm(ks[10], (L, D, IN_WIDTH), D ** -0.5),
        'rpb': nrm(ks[11], (L, NA_HEADS, 2 * NA_WIN_H - 1, 2 * NA_WIN_W - 1), 0.5),
        'gqa_q_norm': gain(ks[12], (L, HEAD_DIM)),
        'gqa_k_norm': gain(ks[13], (L, HEAD_DIM)),
        'mla_q_norm': gain(ks[14], (L, MLA_Q_RANK)),
        'mla_kv_norm': gain(ks[15], (L, MLA_KV_RANK)),
        'w_uq': nrm(ks[16], (L, MLA_Q_RANK, MLA_HEADS * (MLA_NOPE + MLA_ROPE)), MLA_Q_RANK ** -0.5),
        'w_ukv': nrm(ks[17], (L, MLA_KV_RANK, MLA_HEADS * (MLA_NOPE + MLA_V)), MLA_KV_RANK ** -0.5),
        'w_br_a': nrm(ks[18], (L, A_W, D), A_W ** -0.5),
        'w_br_b': nrm(ks[19], (L, B_QW, D), B_QW ** -0.5),
        'w_br_c': nrm(ks[20], (L, C_W, D), C_W ** -0.5),
        'w_o': nrm(ks[21], (L, D, D), D ** -0.5),
        'w_ff1': nrm(ks[22], (L, D, D_FF), D ** -0.5),
        'w_ff3': nrm(ks[23], (L, D, D_FF), D ** -0.5),
        'w_ff2': nrm(ks[24], (L, D_FF, D), D_FF ** -0.5),
    }


def reference(x, c, ctx, c_ctx, w_ada, b_ada, g_pre1, g_post1, g_pre2, g_post2, w_in, rpb,
              gqa_q_norm, gqa_k_norm, mla_q_norm, mla_kv_norm, w_uq, w_ukv,
              w_br_a, w_br_b, w_br_c, w_o, w_ff1, w_ff3, w_ff2):
    seq = x.shape[1]
    t = jnp.arange(seq)
    row, col = t // GRID_W, t % GRID_W
    cx = ctx
    silu_c = jax.nn.silu(c)
    silu_cc = jax.nn.silu(c_ctx)
    for l in range(DEPTH):
        ctx_out = l < DEPTH - 1
        mod = jnp.einsum('bd,de->be', silu_c, w_ada[l]) + b_ada[l]
        sh1, sc1, gt1, sh2, sc2, gt2 = [m[:, None, :] for m in jnp.split(mod, 6, axis=-1)]
        mod_c = jnp.einsum('d,de->e', silu_cc, w_ada[l]) + b_ada[l]
        sh1c, sc1c, gt1c, sh2c, sc2c, gt2c = jnp.split(mod_c, 6, axis=-1)
        h = _rms_norm(x, g_pre1[l]) * (1.0 + sc1) + sh1
        hc = _rms_norm(cx, g_pre1[l]) * (1.0 + sc1c) + sh1c
        y, yc = _token_mixer(h, hc, w_in[l], rpb[l], gqa_q_norm[l], gqa_k_norm[l], mla_q_norm[l], mla_kv_norm[l],
                             w_uq[l], w_ukv[l], w_br_a[l], w_br_b[l], w_br_c[l], w_o[l], row, col, ctx_out)
        x = x + gt1 * _rms_norm(y, g_post1[l])
        h2 = _rms_norm(x, g_pre2[l]) * (1.0 + sc2) + sh2
        x = x + gt2 * _rms_norm(_swiglu(h2, w_ff1[l], w_ff3[l], w_ff2[l]), g_post2[l])
        if ctx_out:
            cx = cx + gt1c * _rms_norm(yc, g_post1[l])
            h2c = _rms_norm(cx, g_pre2[l]) * (1.0 + sc2c) + sh2c
            cx = cx + gt2c * _rms_norm(_swiglu(h2c, w_ff1[l], w_ff3[l], w_ff2[l]), g_post2[l])
    return x
```

```python
import functools

import numpy as np
import jax
import jax.numpy as jnp
from jax import lax
from jax.experimental import pallas as pl
from jax.experimental.pallas import tpu as pltpu

F32 = jnp.float32
BF16 = jnp.bfloat16

GRID_W = 64
HEAD_DIM = 128
NA_HEADS = 8
NA_WIN_H = 8
NA_WIN_W = 16
GQA_HEADS = 8
GQA_KV_HEADS = 2
MLA_HEADS = 8
MLA_Q_RANK = 512
MLA_KV_RANK = 512
MLA_NOPE = 128
MLA_ROPE = 64
MLA_V = 128
ROPE_THETA = 10000.0
EPS = 1e-6
NEG_INF = -1e30

LANES = 128
MOD_ROWS = 16
VMEM_LIMIT = 56 * 1024 * 1024

NA_ROWS_PER_BLOCK = 8
NA_SLAB_ROWS = NA_ROWS_PER_BLOCK + NA_WIN_H - 1


def _params(n_axes):
    return pltpu.CompilerParams(dimension_semantics=("arbitrary",) * n_axes,
                                vmem_limit_bytes=VMEM_LIMIT)


def _rms(x, g):
    return x * lax.rsqrt(jnp.mean(x * x, axis=-1, keepdims=True) + EPS) * g


def _rope(y, cos, s_up, s_dn, shift):
    return y * cos + pltpu.roll(y, LANES - shift, 1) * s_up + pltpu.roll(y, shift, 1) * s_dn


def _dot(a, b):
    return jnp.dot(a, b, preferred_element_type=F32)


def _dot_t(a, b):
    return lax.dot_general(a, b, (((1,), (1,)), ((), ())), preferred_element_type=F32)


def _mod_kernel(c_ref, w_ref, b_ref, o_ref):
    s = jax.nn.silu(c_ref[...])
    o_ref[0] = _dot(s.astype(BF16), w_ref[0].astype(BF16)) + b_ref[0]


def _modulation(cc, w_ada, b_ada, tn=1024):
    L, D, N = w_ada.shape
    return pl.pallas_call(
        _mod_kernel,
        grid=(L, N // tn),
        in_specs=[pl.BlockSpec((MOD_ROWS, D), lambda l, n: (0, 0)),
                  pl.BlockSpec((1, D, tn), lambda l, n: (l, 0, n)),
                  pl.BlockSpec((1, 1, tn), lambda l, n: (l, 0, n))],
        out_specs=pl.BlockSpec((1, MOD_ROWS, tn), lambda l, n: (l, 0, n)),
        out_shape=jax.ShapeDtypeStruct((L, MOD_ROWS, N), F32),
        compiler_params=_params(2),
        name="modulation",
    )(cc, w_ada, b_ada.reshape(L, 1, N))


class _Rows:
    def __init__(self, mod, layer, tm, n_lat, seq, batch):
        self.mod, self.layer, self.batch = mod, layer, batch
        self.n_lat_tiles = n_lat // tm
        self.tiles_per_batch = seq // tm
        self.d = mod.shape[-1]

    def spec(self, chunk):
        base = self.layer * MOD_ROWS * 6

        def index(i, *_):
            r = jnp.where(i < self.n_lat_tiles, i // self.tiles_per_batch, self.batch)
            return (base + r * 6 + chunk, 0, 0)
        return pl.BlockSpec((1, 1, self.d), index)


SH1, SC1, GT1, SH2, SC2, GT2 = range(6)


def _prenorm_kernel(x_ref, c_ref, g_ref, sc_ref, sh_ref, xc_ref, h_ref, *, n_lat_tiles):
    i = pl.program_id(0)

    def emit(x):
        xc_ref[...] = x
        h_ref[...] = (_rms(x, g_ref[...]) * (1.0 + sc_ref[0]) + sh_ref[0]).astype(BF16)

    @pl.when(i < n_lat_tiles)
    def _():
        emit(x_ref[...])

    @pl.when(i >= n_lat_tiles)
    def _():
        emit(c_ref[...])


def _prenorm(x2, c2, g, rows, tm):
    n_lat, D = x2.shape
    n_ctx = c2.shape[0]
    T = n_lat + n_ctx
    nl = n_lat // tm
    return pl.pallas_call(
        functools.partial(_prenorm_kernel, n_lat_tiles=nl),
        grid=(T // tm,),
        in_specs=[pl.BlockSpec((tm, D), lambda i: (jnp.minimum(i, nl - 1), 0)),
                  pl.BlockSpec((tm, D), lambda i: (jnp.maximum(i - nl, 0), 0)),
                  pl.BlockSpec((1, D), lambda i: (0, 0)),
                  rows.spec(SC1), rows.spec(SH1)],
        out_specs=[pl.BlockSpec((tm, D), lambda i: (i, 0)),
                   pl.BlockSpec((tm, D), lambda i: (i, 0))],
        out_shape=[jax.ShapeDtypeStruct((T, D), F32), jax.ShapeDtypeStruct((T, D), BF16)],
        compiler_params=_params(1),
        name="prenorm",
    )(x2, c2, g, rows.mod, rows.mod)


def _mm_plain_kernel(a_ref, w_ref, o_ref):
    o_ref[...] = _dot(a_ref[...], w_ref[...]).astype(o_ref.dtype)


def _mm_plain(a, w, tm, tn, name):
    M, K = a.shape
    N = w.shape[1]
    return pl.pallas_call(
        _mm_plain_kernel,
        grid=(M // tm, N // tn),
        in_specs=[pl.BlockSpec((tm, K), lambda i, n: (i, 0)),
                  pl.BlockSpec((K, tn), lambda i, n: (0, n))],
        out_specs=pl.BlockSpec((tm, tn), lambda i, n: (i, n)),
        out_shape=jax.ShapeDtypeStruct((M, N), BF16),
        compiler_params=_params(2),
        name=name,
    )(a, w)


def _rope_spec(tm, n_lat, seq):
    nl, per = n_lat // tm, seq // tm
    return pl.BlockSpec((tm, LANES), lambda i, n: (jnp.where(i < nl, i % per, per), 0))


def _mm_gqa_qk_kernel(a_ref, w_ref, g_ref, cos_ref, su_ref, sd_ref, o_ref, *, heads):
    acc = _dot(a_ref[...], w_ref[...])
    cos, su, sd = cos_ref[...], su_ref[...], sd_ref[...]
    for h in range(heads):
        sl = slice(h * HEAD_DIM, (h + 1) * HEAD_DIM)
        y = _rms(acc[:, sl], g_ref[:, sl])
        o_ref[:, sl] = _rope(y, cos, su, sd, HEAD_DIM // 4).astype(o_ref.dtype)


def _mm_gqa_qk(a, w, gains, tables, tm, tn, n_lat, seq):
    M, K = a.shape
    N = w.shape[1]
    rs = _rope_spec(tm, n_lat, seq)
    return pl.pallas_call(
        functools.partial(_mm_gqa_qk_kernel, heads=tn // HEAD_DIM),
        grid=(M // tm, N // tn),
        in_specs=[pl.BlockSpec((tm, K), lambda i, n: (i, 0)),
                  pl.BlockSpec((K, tn), lambda i, n: (0, n)),
                  pl.BlockSpec((1, tn), lambda i, n: (0, n)),
                  rs, rs, rs],
        out_specs=pl.BlockSpec((tm, tn), lambda i, n: (i, n)),
        out_shape=jax.ShapeDtypeStruct((M, N), BF16),
        compiler_params=_params(2),
        name="inproj_gqa_qk",
    )(a, w, gains, *tables)


def _mm_mla_c_kernel(a_ref, w_ref, gq_ref, gkv_ref, cos_ref, su_ref, sd_ref,
                     cq_ref, ckv_ref, kr_ref):
    acc = _dot(a_ref[...], w_ref[...])
    cq_ref[...] = _rms(acc[:, :MLA_Q_RANK], gq_ref[...]).astype(BF16)
    ckv_ref[...] = _rms(acc[:, MLA_Q_RANK:MLA_Q_RANK + MLA_KV_RANK], gkv_ref[...]).astype(BF16)
    kr = acc[:, MLA_Q_RANK + MLA_KV_RANK:]
    kr_ref[...] = _rope(kr, cos_ref[...], su_ref[...], sd_ref[...], MLA_ROPE // 4).astype(BF16)


def _mm_mla_c(a, w, gq, gkv, tables, tm, n_lat, seq):
    M, K = a.shape
    N = w.shape[1]
    rs = _rope_spec(tm, n_lat, seq)
    return pl.pallas_call(
        _mm_mla_c_kernel,
        grid=(M // tm, 1),
        in_specs=[pl.BlockSpec((tm, K), lambda i, n: (i, 0)),
                  pl.BlockSpec((K, N), lambda i, n: (0, 0)),
                  pl.BlockSpec((1, MLA_Q_RANK), lambda i, n: (0, 0)),
                  pl.BlockSpec((1, MLA_KV_RANK), lambda i, n: (0, 0)),
                  rs, rs, rs],
        out_specs=[pl.BlockSpec((tm, MLA_Q_RANK), lambda i, n: (i, 0)),
                   pl.BlockSpec((tm, MLA_KV_RANK), lambda i, n: (i, 0)),
                   pl.BlockSpec((tm, LANES), lambda i, n: (i, 0))],
        out_shape=[jax.ShapeDtypeStruct((M, MLA_Q_RANK), BF16),
                   jax.ShapeDtypeStruct((M, MLA_KV_RANK), BF16),
                   jax.ShapeDtypeStruct((M, LANES), BF16)],
        compiler_params=_params(2),
        name="inproj_mla_c",
    )(a, w, gq, gkv, *tables)


def _mm_mla_q_kernel(a_ref, w_ref, cos_ref, su_ref, sd_ref, o_ref, *, heads):
    acc = _dot(a_ref[...], w_ref[...])
    cos, su, sd = cos_ref[...], su_ref[...], sd_ref[...]
    for h in range(heads):
        lo = h * 2 * LANES
        o_ref[:, lo:lo + LANES] = acc[:, lo:lo + LANES].astype(BF16)
        y = _rope(acc[:, lo + LANES:lo + 2 * LANES], cos, su, sd, MLA_ROPE // 4)
        o_ref[:, lo + LANES:lo + 2 * LANES] = y.astype(BF16)


def _mm_mla_q(a, w, tables, tm, tn, n_lat, seq):
    M, K = a.shape
    N = w.shape[1]
    rs = _rope_spec(tm, n_lat, seq)
    return pl.pallas_call(
        functools.partial(_mm_mla_q_kernel, heads=tn // (2 * LANES)),
        grid=(M // tm, N // tn),
        in_specs=[pl.BlockSpec((tm, K), lambda i, n: (i, 0)),
                  pl.BlockSpec((K, tn), lambda i, n: (0, n)),
                  rs, rs, rs],
        out_specs=pl.BlockSpec((tm, tn), lambda i, n: (i, n)),
        out_shape=jax.ShapeDtypeStruct((M, N), BF16),
        compiler_params=_params(2),
        name="mla_q_up",
    )(a, w, *tables)


def _softmax_pv(scores, values, scale_log2e):
    m = scores[0].max(axis=-1, keepdims=True)
    for s in scores[1:]:
        m = jnp.maximum(m, s.max(axis=-1, keepdims=True))
    den, o = None, None
    for s, v in zip(scores, values):
        e = jnp.exp2((s - m) * scale_log2e)
        d = e.sum(axis=-1, keepdims=True)
        pv = _dot(e.astype(BF16), v)
        den = d if den is None else den + d
        o = pv if o is None else o + pv
    return o / den


def _attn_kernel(*refs, groups, scale, has_lat, has_rope):
    it = iter(refs)
    q_ref = next(it)
    keys = []
    for _ in range(2 if has_lat else 1):
        k_ref = next(it)
        kr_ref = next(it) if has_rope else None
        v_ref = next(it)
        keys.append((k_ref, kr_ref, v_ref))
    o_ref = next(it)
    c = scale * float(np.log2(np.e))
    width = q_ref.shape[1] // groups
    for g in range(groups):
        q = q_ref[:, g * width:g * width + HEAD_DIM]
        qr = q_ref[:, g * width + HEAD_DIM:(g + 1) * width] if has_rope else None
        scores, values = [], []
        for k_ref, kr_ref, v_ref in keys:
            s = _dot_t(q, k_ref[...])
            if has_rope:
                s = s + _dot_t(qr, kr_ref[...])
            scores.append(s)
            values.append(v_ref[...])
        o = _softmax_pv(scores, values, c)
        o_ref[:, g * HEAD_DIM:(g + 1) * HEAD_DIM] = o.astype(o_ref.dtype)


def _attention(q_arr, q_col, q_width, k_arr, k_col, v_arr, v_col, kr_arr, *, groups, heads, scale,
               batch, seq, ctx, tq, ctx_queries, out_rows, out_prev=None, name):
    n_lat = batch * seq
    has_rope = kr_arr is not None
    has_lat = not ctx_queries
    cblk = n_lat // ctx
    if ctx_queries:
        tq, nq = ctx, 1
        q_row = lambda b, h, j: cblk + b
    else:
        nq = seq // tq
        q_row = lambda b, h, j: b * nq + j

    in_specs = [pl.BlockSpec((tq, q_width), lambda b, h, j: (q_row(b, h, j), q_col(h)))]
    args = [q_arr]

    def add_keys(rows, row_fn):
        in_specs.append(pl.BlockSpec((rows, HEAD_DIM), lambda b, h, j: (row_fn(b), k_col(h))))
        args.append(k_arr)
        if has_rope:
            in_specs.append(pl.BlockSpec((rows, LANES), lambda b, h, j: (row_fn(b), 0)))
            args.append(kr_arr)
        in_specs.append(pl.BlockSpec((rows, HEAD_DIM), lambda b, h, j: (row_fn(b), v_col(h))))
        args.append(v_arr)

    if has_lat:
        add_keys(seq, lambda b: b)
    add_keys(ctx, lambda b: cblk + b)

    out_w = groups * HEAD_DIM
    aliases = {}
    if out_prev is not None:
        in_specs.append(pl.BlockSpec(memory_space=pl.ANY))
        args.append(out_prev)
        aliases = {len(args) - 1: 0}

    kern = functools.partial(_attn_kernel, groups=groups, scale=scale, has_lat=has_lat,
                             has_rope=has_rope)
    if out_prev is not None:
        kern = functools.partial(_drop_arg_kernel, kern, len(args) - 1)
    return pl.pallas_call(
        kern,
        grid=(batch, heads, nq),
        in_specs=in_specs,
        out_specs=pl.BlockSpec((tq, out_w), lambda b, h, j: (q_row(b, h, j), h)),
        out_shape=jax.ShapeDtypeStruct((out_rows, heads * out_w), BF16),
        input_output_aliases=aliases,
        compiler_params=_params(3),
        name=name,
    )(*args)


def _drop_arg_kernel(kern, idx, *refs):
    kern(*(refs[:idx] + refs[idx + 1:]))


def _na_slab_start(rb, rows):
    return int(np.clip(rb * NA_ROWS_PER_BLOCK - NA_WIN_H // 2, 0, rows - NA_SLAB_ROWS))


def _na_kernel(rpb_ref, q_ref, k_ref, v_ref, kc_ref, vc_ref, o_ref, bias_ref, *, rows, scale):
    h = pl.program_id(0)
    n_blocks = rows // NA_ROWS_PER_BLOCK
    bw = 2 * NA_WIN_W - 1

    @pl.when(pl.program_id(1) == 0)
    def _build_bias():
        qc = lax.broadcasted_iota(jnp.int32, (GRID_W, GRID_W), 0)
        kc = lax.broadcasted_iota(jnp.int32, (GRID_W, GRID_W), 1)
        c0 = jnp.clip(qc - NA_WIN_W // 2, 0, GRID_W - NA_WIN_W)
        in_win = (kc >= c0) & (kc < c0 + NA_WIN_W)
        dc = jnp.clip(kc - qc, -(NA_WIN_W - 1), NA_WIN_W - 1) + (NA_WIN_W - 1)
        tiles = []
        for dr in range(2 * NA_WIN_H - 1):
            w = jnp.zeros((GRID_W, GRID_W), F32)
            for j in range(bw):
                w = jnp.where(dc == j, rpb_ref[h, dr * bw + j], w)
            tiles.append(jnp.where(in_win, w, NEG_INF))
        masked = jnp.full((GRID_W, GRID_W), NEG_INF, F32)
        for rb in range(n_blocks):
            u0 = _na_slab_start(rb, rows)
            for i in range(NA_ROWS_PER_BLOCK):
                r = rb * NA_ROWS_PER_BLOCK + i
                r0 = int(np.clip(r - NA_WIN_H // 2, 0, rows - NA_WIN_H))
                for u in range(NA_SLAB_ROWS):
                    kr = u0 + u
                    t = tiles[kr - r + NA_WIN_H - 1] if r0 <= kr < r0 + NA_WIN_H else masked
                    bias_ref[rb, i * GRID_W:(i + 1) * GRID_W, u * GRID_W:(u + 1) * GRID_W] = t

    c = float(np.log2(np.e))
    qb = NA_ROWS_PER_BLOCK * GRID_W
    for rb in range(n_blocks):
        k0 = _na_slab_start(rb, rows) * GRID_W
        k1 = k0 + NA_SLAB_ROWS * GRID_W
        q = q_ref[rb * qb:(rb + 1) * qb, :]
        s_win = _dot_t(q, k_ref[k0:k1, :]) * scale + bias_ref[rb]
        s_ctx = _dot_t(q, kc_ref[...]) * scale
        o = _softmax_pv([s_win, s_ctx], [v_ref[k0:k1, :], vc_ref[...]], c)
        o_ref[rb * qb:(rb + 1) * qb, :] = o.astype(o_ref.dtype)


def _na_attention(qkv, rpb, *, batch, seq, ctx, out_rows):
    n_lat = batch * seq
    rows = seq // GRID_W
    n_blocks = rows // NA_ROWS_PER_BLOCK
    cblk = n_lat // ctx
    H = NA_HEADS
    return pl.pallas_call(
        functools.partial(_na_kernel, rows=rows, scale=HEAD_DIM ** -0.5),
        grid=(H, batch),
        in_specs=[pl.BlockSpec(memory_space=pltpu.SMEM),
                  pl.BlockSpec((seq, HEAD_DIM), lambda h, b: (b, h)),
                  pl.BlockSpec((seq, HEAD_DIM), lambda h, b: (b, H + h)),
                  pl.BlockSpec((seq, HEAD_DIM), lambda h, b: (b, 2 * H + h)),
                  pl.BlockSpec((ctx, HEAD_DIM), lambda h, b: (cblk + b, H + h)),
                  pl.BlockSpec((ctx, HEAD_DIM), lambda h, b: (cblk + b, 2 * H + h))],
        out_specs=pl.BlockSpec((seq, HEAD_DIM), lambda h, b: (b, h)),
        out_shape=jax.ShapeDtypeStruct((out_rows, H * HEAD_DIM), BF16),
        scratch_shapes=[pltpu.VMEM((n_blocks, NA_ROWS_PER_BLOCK * GRID_W, NA_SLAB_ROWS * GRID_W), F32)],
        compiler_params=_params(2),
        name="na_attention",
    )(rpb.reshape(H, -1), qkv, qkv, qkv, qkv, qkv)


def _merge_kernel(h_ref, oa_ref, ob_ref, oc_ref, wga_ref, wgb_ref, wgc_ref,
                  wa_ref, wb_ref, wc_ref, y_ref):
    h = h_ref[...]
    y = (jax.nn.sigmoid(_dot(h, wga_ref[...])) * _dot(oa_ref[...], wa_ref[...])
         + jax.nn.sigmoid(_dot(h, wgb_ref[...])) * _dot(ob_ref[...], wb_ref[...])
         + jax.nn.sigmoid(_dot(h, wgc_ref[...])) * _dot(oc_ref[...], wc_ref[...]))
    y_ref[...] = y.astype(BF16)


def _merge(h, o_a, o_b, o_c, wg, wbr, m_rows, tm, tn):
    D = h.shape[1]
    W = o_a.shape[1]
    act = lambda k: pl.BlockSpec((tm, k), lambda i, n: (i, 0))
    wsp = lambda k: pl.BlockSpec((k, tn), lambda i, n: (0, n))
    return pl.pallas_call(
        _merge_kernel,
        grid=(m_rows // tm, D // tn),
        in_specs=[act(D), act(W), act(W), act(W), wsp(D), wsp(D), wsp(D), wsp(W), wsp(W), wsp(W)],
        out_specs=pl.BlockSpec((tm, tn), lambda i, n: (i, n)),
        out_shape=jax.ShapeDtypeStruct((m_rows, D), BF16),
        compiler_params=_params(2),
        name="merge",
    )(h, o_a, o_b, o_c, *wg, *wbr)


def _wo_kernel(y_ref, w_ref, x_ref, gpost_ref, gt_ref, gpre_ref, sc_ref, sh_ref, xo_ref, h_ref):
    z = _dot(y_ref[...], w_ref[...])
    xn = x_ref[...] + gt_ref[0] * _rms(z, gpost_ref[...])
    xo_ref[...] = xn
    h_ref[...] = (_rms(xn, gpre_ref[...]) * (1.0 + sc_ref[0]) + sh_ref[0]).astype(BF16)


def _wo(y, w_o, xc, g_post, g_pre2, rows, tm):
    M, D = y.shape
    row = lambda i: (i, 0)
    fixed = lambda i: (0, 0)
    return pl.pallas_call(
        _wo_kernel,
        grid=(M // tm,),
        in_specs=[pl.BlockSpec((tm, D), row),
                  pl.BlockSpec((D, D), fixed, pipeline_mode=pl.Buffered(1)),
                  pl.BlockSpec((tm, D), row),
                  pl.BlockSpec((1, D), fixed),
                  rows.spec(GT1),
                  pl.BlockSpec((1, D), fixed),
                  rows.spec(SC2), rows.spec(SH2)],
        out_specs=[pl.BlockSpec((tm, D), row), pl.BlockSpec((tm, D), row)],
        out_shape=[jax.ShapeDtypeStruct((M, D), F32), jax.ShapeDtypeStruct((M, D), BF16)],
        compiler_params=_params(1),
        name="out_proj",
    )(y, w_o, xc, g_post, rows.mod, g_pre2, rows.mod, rows.mod)


def _ffn_up_kernel(h_ref, w1_ref, w3_ref, u_ref):
    h = h_ref[...]
    u_ref[...] = (jax.nn.silu(_dot(h, w1_ref[...])) * _dot(h, w3_ref[...])).astype(BF16)


def _ffn_up(h2, w1, w3, tm, tn):
    M, D = h2.shape
    N = w1.shape[1]
    wsp = pl.BlockSpec((D, tn), lambda i, n: (0, n))
    return pl.pallas_call(
        _ffn_up_kernel,
        grid=(M // tm, N // tn),
        in_specs=[pl.BlockSpec((tm, D), lambda i, n: (i, 0)), wsp, wsp],
        out_specs=pl.BlockSpec((tm, tn), lambda i, n: (i, n)),
        out_shape=jax.ShapeDtypeStruct((M, N), BF16),
        compiler_params=_params(2),
        name="ffn_up",
    )(h2, w1, w3)


def _ffn_down_kernel(u_ref, w_ref, x_ref, gpost_ref, gt_ref, *rest, n_k, emit_h):
    if emit_h:
        gpre_ref, sc_ref, sh_ref, xo_ref, h_ref, acc_ref = rest
    else:
        xo_ref, acc_ref = rest
    k = pl.program_id(1)
    part = _dot(u_ref[...], w_ref[...])

    @pl.when(k == 0)
    def _():
        acc_ref[...] = part

    @pl.when(k > 0)
    def _():
        acc_ref[...] += part

    @pl.when(k == n_k - 1)
    def _():
        xn = x_ref[...] + gt_ref[0] * _rms(acc_ref[...], gpost_ref[...])
        xo_ref[...] = xn
        if emit_h:
            h_ref[...] = (_rms(xn, gpre_ref[...]) * (1.0 + sc_ref[0]) + sh_ref[0]).astype(BF16)


def _ffn_down(u, w2, xc, g_post, rows, tm, tk, next_pre=None):
    M, K = u.shape
    D = w2.shape[1]
    n_k = K // tk
    row = lambda i, k: (i, 0)
    fixed = lambda i, k: (0, 0)
    in_specs = [pl.BlockSpec((tm, tk), lambda i, k: (i, k)),
                pl.BlockSpec((tk, D), lambda i, k: (k, 0)),
                pl.BlockSpec((tm, D), row),
                pl.BlockSpec((1, D), fixed),
                rows.spec(GT2)]
    args = [u, w2, xc, g_post, rows.mod]
    out_specs = [pl.BlockSpec((tm, D), row)]
    out_shape = [jax.ShapeDtypeStruct((M, D), F32)]
    if next_pre is not None:
        g_pre, nrows = next_pre
        in_specs += [pl.BlockSpec((1, D), fixed), nrows.spec(SC1), nrows.spec(SH1)]
        args += [g_pre, nrows.mod, nrows.mod]
        out_specs.append(pl.BlockSpec((tm, D), row))
        out_shape.append(jax.ShapeDtypeStruct((M, D), BF16))
    return pl.pallas_call(
        functools.partial(_ffn_down_kernel, n_k=n_k, emit_h=next_pre is not None),
        grid=(M // tm, n_k),
        in_specs=in_specs,
        out_specs=out_specs,
        out_shape=out_shape,
        scratch_shapes=[pltpu.VMEM((tm, D), F32)],
        compiler_params=_params(2),
        name="ffn_down",
    )(*args)


def _rope_tables(seq, pad_rows, rot_dim):
    t = jnp.arange(seq)
    half = rot_dim // 2
    freqs = ROPE_THETA ** (-jnp.arange(0, half, 2, dtype=F32) / half)
    zeros = jnp.zeros((seq, half // 2), F32)
    cos, s_up, s_dn = [], [], []
    for pos in (t // GRID_W, t % GRID_W):
        ang = pos.astype(F32)[:, None] * freqs[None, :]
        c, s = jnp.cos(ang), jnp.sin(ang)
        cos += [c, c]
        s_up += [-s, zeros]
        s_dn += [zeros, s]

    def finish(parts, ident):
        tab = jnp.concatenate(parts, axis=-1)
        tab = jnp.pad(tab, ((0, 0), (0, LANES - rot_dim)))
        return jnp.concatenate([tab, jnp.full((pad_rows, LANES), ident, F32)], axis=0)
    return finish(cos, 1.0), finish(s_up, 0.0), finish(s_dn, 0.0)


def kernel(x, c, ctx, c_ctx, w_ada, b_ada, g_pre1, g_post1, g_pre2, g_post2, w_in, rpb,
           gqa_q_norm, gqa_k_norm, mla_q_norm, mla_kv_norm, w_uq, w_ukv,
           w_br_a, w_br_b, w_br_c, w_o, w_ff1, w_ff3, w_ff2):
    B, S, D = x.shape
    C = ctx.shape[1]
    L = w_ada.shape[0]
    n_lat, n_ctx = B * S, B * C
    T = n_lat + n_ctx
    assert B < MOD_ROWS and S // GRID_W == 32 and C % 256 == 0

    A_W = NA_HEADS * HEAD_DIM
    BQ, BKV = GQA_HEADS * HEAD_DIM, GQA_KV_HEADS * HEAD_DIM
    o_bq = 3 * A_W
    o_bk, o_bv = o_bq + BQ, o_bq + BQ + BKV
    o_cq = o_bv + BKV
    o_ckv, o_ckr = o_cq + MLA_Q_RANK, o_cq + MLA_Q_RANK + MLA_KV_RANK
    o_g = o_ckr + MLA_ROPE

    cc = jnp.zeros((MOD_ROWS, D), F32).at[:B].set(c).at[B].set(c_ctx)
    mod = _modulation(cc, w_ada, b_ada).reshape(L * MOD_ROWS * 6, 1, D)

    TM = next(t for t in (1024, 512, 256) if n_ctx % t == 0)
    TM_RES = min(TM, 512)
    gqa_tabs = _rope_tables(S, TM, HEAD_DIM)
    mla_tabs = _rope_tables(S, TM, MLA_ROPE)
    row2 = lambda v: v.reshape(1, -1)

    xc = h = None
    for l in range(L):
        last = l == L - 1
        wl = w_in[l]
        w_a = jnp.concatenate([wl[:, :o_bq], wl[:, o_bv:o_cq]], axis=1).astype(BF16)
        w_b = wl[:, o_bq:o_bv].astype(BF16)
        w_c = jnp.pad(wl[:, o_cq:o_g], ((0, 0), (0, LANES - MLA_ROPE))).astype(BF16)
        w_g = [wl[:, o_g + j * D:o_g + (j + 1) * D].astype(BF16) for j in range(3)]
        wq = w_uq[l].reshape(MLA_Q_RANK, MLA_HEADS, MLA_NOPE + MLA_ROPE)
        wq = jnp.pad(wq, ((0, 0), (0, 0), (0, 2 * LANES - MLA_NOPE - MLA_ROPE)))
        wq = wq.reshape(MLA_Q_RANK, MLA_HEADS * 2 * LANES).astype(BF16)
        wkv = w_ukv[l].reshape(MLA_KV_RANK, MLA_HEADS, MLA_NOPE + MLA_V)
        wkv = jnp.concatenate([wkv[:, :, :MLA_NOPE].reshape(MLA_KV_RANK, -1),
                               wkv[:, :, MLA_NOPE:].reshape(MLA_KV_RANK, -1)], axis=1).astype(BF16)
        wbr = [w.astype(BF16) for w in (w_br_a[l], w_br_b[l], w_br_c[l])]
        wo = w_o[l].astype(BF16)
        w1, w3, w2 = w_ff1[l].astype(BF16), w_ff3[l].astype(BF16), w_ff2[l].astype(BF16)
        gains_b = jnp.concatenate([jnp.tile(gqa_q_norm[l], GQA_HEADS),
                                   jnp.tile(gqa_k_norm[l], GQA_KV_HEADS)]).reshape(1, -1)

        rows_res = _Rows(mod, l, TM_RES, n_lat, S, B)
        if l == 0:
            xc, h = _prenorm(x.reshape(n_lat, D), ctx.reshape(n_ctx, D), row2(g_pre1[l]),
                             rows_res, TM_RES)

        qkv = _mm_plain(h, w_a, TM, w_a.shape[1] // 2, "inproj_plain")
        qk_b = _mm_gqa_qk(h, w_b, gains_b, gqa_tabs, TM, w_b.shape[1] // 2, n_lat, S)
        cq, ckv, k_rope = _mm_mla_c(h, w_c, row2(mla_q_norm[l]), row2(mla_kv_norm[l]),
                                    mla_tabs, TM, n_lat, S)
        q_c = _mm_mla_q(cq, wq, mla_tabs, TM, wq.shape[1] // 2, n_lat, S)
        kv_c = _mm_plain(ckv, wkv, TM, wkv.shape[1] // 2, "mla_kv_up")

        H = NA_HEADS
        scale_b = HEAD_DIM ** -0.5
        scale_c = (MLA_NOPE + MLA_ROPE) ** -0.5
        m_rows = n_lat if last else T
        common = dict(batch=B, seq=S, ctx=C, tq=512, out_rows=m_rows)
        grp = GQA_HEADS // GQA_KV_HEADS
        o_a = _na_attention(qkv, rpb[l], batch=B, seq=S, ctx=C, out_rows=m_rows)
        gqa = dict(q_arr=qk_b, q_col=lambda h: h, q_width=grp * HEAD_DIM,
                   k_arr=qk_b, k_col=lambda h: GQA_HEADS + h,
                   v_arr=qkv, v_col=lambda h: 3 * H + h, kr_arr=None,
                   groups=grp, heads=GQA_KV_HEADS, scale=scale_b, **common)
        mla = dict(q_arr=q_c, q_col=lambda h: h, q_width=2 * LANES,
                   k_arr=kv_c, k_col=lambda h: h, v_arr=kv_c, v_col=lambda h: MLA_HEADS + h,
                   kr_arr=k_rope, groups=1, heads=MLA_HEADS, scale=scale_c, **common)
        o_b = _attention(**gqa, ctx_queries=False, name="gqa_attention")
        o_c = _attention(**mla, ctx_queries=False, name="mla_attention")
        if not last:
            na_c = dict(q_arr=qkv, q_col=lambda h: h, q_width=HEAD_DIM,
                        k_arr=qkv, k_col=lambda h: H + h, v_arr=qkv, v_col=lambda h: 2 * H + h,
                        kr_arr=None, groups=1, heads=H, scale=scale_b, **common)
            o_a = _attention(**na_c, ctx_queries=True, out_prev=o_a, name="na_attention_ctx")
            o_b = _attention(**gqa, ctx_queries=True, out_prev=o_b, name="gqa_attention_ctx")
            o_c = _attention(**mla, ctx_queries=True, out_prev=o_c, name="mla_attention_ctx")

        y = _merge(h, o_a, o_b, o_c, w_g, wbr, m_rows, TM_RES, 512)
        xc, h2 = _wo(y, wo, xc, row2(g_post1[l]), row2(g_pre2[l]), rows_res, TM_RES)
        u = _ffn_up(h2, w1, w3, TM, 512)
        if last:
            xc, = _ffn_down(u, w2, xc, row2(g_post2[l]), rows_res, TM_RES, w2.shape[0] // 4)
        else:
            nxt = (row2(g_pre1[l + 1]), _Rows(mod, l + 1, TM_RES, n_lat, S, B))
            xc, h = _ffn_down(u, w2, xc, row2(g_post2[l]), rows_res, TM_RES, w2.shape[0] // 4, nxt)
    return xc.reshape(B, S, D)
```

```python
import functools

import numpy as np
import jax
import jax.numpy as jnp
from jax import lax
from jax.experimental import pallas as pl
from jax.experimental.pallas import tpu as pltpu

F32 = jnp.float32
BF16 = jnp.bfloat16

GRID_W = 64
HEAD_DIM = 128
NA_HEADS = 8
NA_WIN_H = 8
NA_WIN_W = 16
GQA_HEADS = 8
GQA_KV_HEADS = 2
MLA_HEADS = 8
MLA_Q_RANK = 512
MLA_KV_RANK = 512
MLA_NOPE = 128
MLA_ROPE = 64
MLA_V = 128
ROPE_THETA = 10000.0
EPS = 1e-6
NEG_INF = -1e30

LANES = 128
MOD_ROWS = 16
VMEM_LIMIT = 56 * 1024 * 1024

NA_ROWS_PER_BLOCK = 8
NA_SLAB_ROWS = NA_ROWS_PER_BLOCK + NA_WIN_H - 1


def _params(n_axes):
    return pltpu.CompilerParams(dimension_semantics=("arbitrary",) * n_axes,
                                vmem_limit_bytes=VMEM_LIMIT)


def _rms(x, g):
    return x * lax.rsqrt(jnp.mean(x * x, axis=-1, keepdims=True) + EPS) * g


def _rope(y, cos, sin):
    return y * cos + pltpu.roll(y, LANES // 2, 1) * sin


def _dot(a, b):
    return jnp.dot(a, b, preferred_element_type=F32)


def _dot_t(a, b):
    return lax.dot_general(a, b, (((1,), (1,)), ((), ())), preferred_element_type=F32)


def _mod_kernel(c_ref, w_ref, b_ref, o_ref):
    s = jax.nn.silu(c_ref[...])
    o_ref[0] = _dot(s.astype(BF16), w_ref[0].astype(BF16)) + b_ref[0]


def _modulation(cc, w_ada, b_ada, tn=1024):
    L, D, N = w_ada.shape
    return pl.pallas_call(
        _mod_kernel,
        grid=(L, N // tn),
        in_specs=[pl.BlockSpec((MOD_ROWS, D), lambda l, n: (0, 0)),
                  pl.BlockSpec((1, D, tn), lambda l, n: (l, 0, n)),
                  pl.BlockSpec((1, 1, tn), lambda l, n: (l, 0, n))],
        out_specs=pl.BlockSpec((1, MOD_ROWS, tn), lambda l, n: (l, 0, n)),
        out_shape=jax.ShapeDtypeStruct((L, MOD_ROWS, N), F32),
        compiler_params=_params(2),
        name="modulation",
    )(cc, w_ada, b_ada.reshape(L, 1, N))


class _Rows:
    def __init__(self, mod, layer, tm, n_lat, seq, batch):
        self.mod, self.layer, self.batch = mod, layer, batch
        self.n_lat_tiles = n_lat // tm
        self.tiles_per_batch = seq // tm
        self.d = mod.shape[-1]

    def spec(self, chunk):
        base = self.layer * MOD_ROWS * 6

        def index(i, *_):
            r = jnp.where(i < self.n_lat_tiles, i // self.tiles_per_batch, self.batch)
            return (base + r * 6 + chunk, 0, 0)
        return pl.BlockSpec((1, 1, self.d), index)


SH1, SC1, GT1, SH2, SC2, GT2 = range(6)


def _prenorm_kernel(x_ref, c_ref, g_ref, sc_ref, sh_ref, xc_ref, h_ref, *, n_lat_tiles):
    i = pl.program_id(0)

    def emit(x):
        xc_ref[...] = x
        h_ref[...] = (_rms(x, g_ref[...]) * (1.0 + sc_ref[0]) + sh_ref[0]).astype(BF16)

    @pl.when(i < n_lat_tiles)
    def _():
        emit(x_ref[...])

    @pl.when(i >= n_lat_tiles)
    def _():
        emit(c_ref[...])


def _prenorm(x2, c2, g, rows, tm):
    n_lat, D = x2.shape
    n_ctx = c2.shape[0]
    T = n_lat + n_ctx
    nl = n_lat // tm
    return pl.pallas_call(
        functools.partial(_prenorm_kernel, n_lat_tiles=nl),
        grid=(T // tm,),
        in_specs=[pl.BlockSpec((tm, D), lambda i: (jnp.minimum(i, nl - 1), 0)),
                  pl.BlockSpec((tm, D), lambda i: (jnp.maximum(i - nl, 0), 0)),
                  pl.BlockSpec((1, D), lambda i: (0, 0)),
                  rows.spec(SC1), rows.spec(SH1)],
        out_specs=[pl.BlockSpec((tm, D), lambda i: (i, 0)),
                   pl.BlockSpec((tm, D), lambda i: (i, 0))],
        out_shape=[jax.ShapeDtypeStruct((T, D), F32), jax.ShapeDtypeStruct((T, D), BF16)],
        compiler_params=_params(1),
        name="prenorm",
    )(x2, c2, g, rows.mod, rows.mod)


def _mm_plain_kernel(a_ref, w_ref, o_ref):
    o_ref[...] = _dot(a_ref[...], w_ref[...]).astype(o_ref.dtype)


def _mm_plain(a, w, tm, tn, name):
    M, K = a.shape
    N = w.shape[1]
    return pl.pallas_call(
        _mm_plain_kernel,
        grid=(M // tm, N // tn),
        in_specs=[pl.BlockSpec((tm, K), lambda i, n: (i, 0)),
                  pl.BlockSpec((K, tn), lambda i, n: (0, n))],
        out_specs=pl.BlockSpec((tm, tn), lambda i, n: (i, n)),
        out_shape=jax.ShapeDtypeStruct((M, N), BF16),
        compiler_params=_params(2),
        name=name,
    )(a, w)


def _rope_spec(tm, n_lat, seq):
    nl, per = n_lat // tm, seq // tm
    return pl.BlockSpec((tm, LANES), lambda i, n: (jnp.where(i < nl, i % per, per), 0))


def _mm_gqa_qk_kernel(a_ref, w_ref, g_ref, cos_ref, sin_ref, o_ref, *, heads):
    acc = _dot(a_ref[...], w_ref[...])
    cos, sin = cos_ref[...], sin_ref[...]
    for h in range(heads):
        sl = slice(h * HEAD_DIM, (h + 1) * HEAD_DIM)
        y = _rms(acc[:, sl], g_ref[:, sl])
        o_ref[:, sl] = _rope(y, cos, sin).astype(o_ref.dtype)


def _mm_gqa_qk(a, w, gains, tables, tm, tn, n_lat, seq):
    M, K = a.shape
    N = w.shape[1]
    rs = _rope_spec(tm, n_lat, seq)
    return pl.pallas_call(
        functools.partial(_mm_gqa_qk_kernel, heads=tn // HEAD_DIM),
        grid=(M // tm, N // tn),
        in_specs=[pl.BlockSpec((tm, K), lambda i, n: (i, 0)),
                  pl.BlockSpec((K, tn), lambda i, n: (0, n)),
                  pl.BlockSpec((1, tn), lambda i, n: (0, n)),
                  rs, rs],
        out_specs=pl.BlockSpec((tm, tn), lambda i, n: (i, n)),
        out_shape=jax.ShapeDtypeStruct((M, N), BF16),
        compiler_params=_params(2),
        name="inproj_gqa_qk",
    )(a, w, gains, *tables)


def _mm_mla_c_kernel(a_ref, w_ref, gq_ref, gkv_ref, cos_ref, sin_ref, cq_ref, ckv_ref, kr_ref):
    acc = _dot(a_ref[...], w_ref[...])
    cq_ref[...] = _rms(acc[:, :MLA_Q_RANK], gq_ref[...]).astype(BF16)
    ckv_ref[...] = _rms(acc[:, MLA_Q_RANK:MLA_Q_RANK + MLA_KV_RANK], gkv_ref[...]).astype(BF16)
    kr = acc[:, MLA_Q_RANK + MLA_KV_RANK:]
    kr_ref[...] = _rope(kr, cos_ref[...], sin_ref[...]).astype(BF16)


def _mm_mla_c(a, w, gq, gkv, tables, tm, n_lat, seq):
    M, K = a.shape
    N = w.shape[1]
    rs = _rope_spec(tm, n_lat, seq)
    return pl.pallas_call(
        _mm_mla_c_kernel,
        grid=(M // tm, 1),
        in_specs=[pl.BlockSpec((tm, K), lambda i, n: (i, 0)),
                  pl.BlockSpec((K, N), lambda i, n: (0, 0)),
                  pl.BlockSpec((1, MLA_Q_RANK), lambda i, n: (0, 0)),
                  pl.BlockSpec((1, MLA_KV_RANK), lambda i, n: (0, 0)),
                  rs, rs],
        out_specs=[pl.BlockSpec((tm, MLA_Q_RANK), lambda i, n: (i, 0)),
                   pl.BlockSpec((tm, MLA_KV_RANK), lambda i, n: (i, 0)),
                   pl.BlockSpec((tm, LANES), lambda i, n: (i, 0))],
        out_shape=[jax.ShapeDtypeStruct((M, MLA_Q_RANK), BF16),
                   jax.ShapeDtypeStruct((M, MLA_KV_RANK), BF16),
                   jax.ShapeDtypeStruct((M, LANES), BF16)],
        compiler_params=_params(2),
        name="inproj_mla_c",
    )(a, w, gq, gkv, *tables)


def _mm_mla_q_kernel(a_ref, w_ref, cos_ref, sin_ref, o_ref, *, heads):
    acc = _dot(a_ref[...], w_ref[...])
    cos, sin = cos_ref[...], sin_ref[...]
    for h in range(heads):
        lo = h * 2 * LANES
        o_ref[:, lo:lo + LANES] = acc[:, lo:lo + LANES].astype(BF16)
        y = _rope(acc[:, lo + LANES:lo + 2 * LANES], cos, sin)
        o_ref[:, lo + LANES:lo + 2 * LANES] = y.astype(BF16)


def _mm_mla_q(a, w, tables, tm, tn, n_lat, seq):
    M, K = a.shape
    N = w.shape[1]
    rs = _rope_spec(tm, n_lat, seq)
    return pl.pallas_call(
        functools.partial(_mm_mla_q_kernel, heads=tn // (2 * LANES)),
        grid=(M // tm, N // tn),
        in_specs=[pl.BlockSpec((tm, K), lambda i, n: (i, 0)),
                  pl.BlockSpec((K, tn), lambda i, n: (0, n)),
                  rs, rs],
        out_specs=pl.BlockSpec((tm, tn), lambda i, n: (i, n)),
        out_shape=jax.ShapeDtypeStruct((M, N), BF16),
        compiler_params=_params(2),
        name="mla_q_up",
    )(a, w, *tables)


def _softmax_pv(scores, values, scale_log2e):
    m = scores[0].max(axis=-1, keepdims=True)
    for s in scores[1:]:
        m = jnp.maximum(m, s.max(axis=-1, keepdims=True))
    den, o = None, None
    for s, v in zip(scores, values):
        e = jnp.exp2((s - m) * scale_log2e)
        d = e.sum(axis=-1, keepdims=True)
        pv = _dot(e.astype(BF16), v)
        den = d if den is None else den + d
        o = pv if o is None else o + pv
    return o / den


def _attn_kernel(*refs, groups, scale, has_lat, has_rope, sub):
    it = iter(refs)
    q_ref = next(it)
    keys = []
    for _ in range(2 if has_lat else 1):
        k_ref = next(it)
        kr_ref = next(it) if has_rope else None
        v_ref = next(it)
        keys.append([k_ref, kr_ref, v_ref])
    o_ref = next(it)
    if has_rope:
        for entry in keys:
            entry.append(next(it))

        @pl.when(pl.program_id(2) == 0)
        def _():
            for k_ref, kr_ref, _, kcat_ref in keys:
                kcat_ref[:, :HEAD_DIM] = k_ref[...]
                kcat_ref[:, HEAD_DIM:] = kr_ref[...]
    c = scale * float(np.log2(np.e))
    width = q_ref.shape[1] // groups
    for g in range(groups):
        for t in range(q_ref.shape[0] // sub):
            rows = slice(t * sub, (t + 1) * sub)
            q = q_ref[rows, g * width:(g + 1) * width]
            scores = [_dot_t(q, (entry[3] if has_rope else entry[0])[...]) for entry in keys]
            o = _softmax_pv(scores, [entry[2][...] for entry in keys], c)
            o_ref[rows, g * HEAD_DIM:(g + 1) * HEAD_DIM] = o.astype(o_ref.dtype)


def _attention(q_arr, q_col, q_width, k_arr, k_col, v_arr, v_col, kr_arr, *, groups, heads, scale,
               batch, seq, ctx, tq, sub, ctx_queries, out_rows, out_prev=None, name):
    n_lat = batch * seq
    has_rope = kr_arr is not None
    has_lat = not ctx_queries
    cblk = n_lat // ctx
    if ctx_queries:
        tq, nq = ctx, 1
        q_row = lambda b, h, j: cblk + b
    else:
        nq = seq // tq
        q_row = lambda b, h, j: b * nq + j

    in_specs = [pl.BlockSpec((tq, q_width), lambda b, h, j: (q_row(b, h, j), q_col(h)))]
    args = [q_arr]

    def add_keys(rows, row_fn):
        in_specs.append(pl.BlockSpec((rows, HEAD_DIM), lambda b, h, j: (row_fn(b), k_col(h))))
        args.append(k_arr)
        if has_rope:
            in_specs.append(pl.BlockSpec((rows, LANES), lambda b, h, j: (row_fn(b), 0)))
            args.append(kr_arr)
        in_specs.append(pl.BlockSpec((rows, HEAD_DIM), lambda b, h, j: (row_fn(b), v_col(h))))
        args.append(v_arr)

    if has_lat:
        add_keys(seq, lambda b: b)
    add_keys(ctx, lambda b: cblk + b)

    out_w = groups * HEAD_DIM
    aliases = {}
    if out_prev is not None:
        in_specs.append(pl.BlockSpec(memory_space=pl.ANY))
        args.append(out_prev)
        aliases = {len(args) - 1: 0}

    kern = functools.partial(_attn_kernel, groups=groups, scale=scale, has_lat=has_lat,
                             has_rope=has_rope, sub=min(sub, tq))
    if out_prev is not None:
        kern = functools.partial(_drop_arg_kernel, kern, len(args) - 1)
    scratch = []
    if has_rope:
        key_rows = ([seq] if has_lat else []) + [ctx]
        scratch = [pltpu.VMEM((r, HEAD_DIM + LANES), BF16) for r in key_rows]
    return pl.pallas_call(
        kern,
        grid=(batch, heads, nq),
        in_specs=in_specs,
        out_specs=pl.BlockSpec((tq, out_w), lambda b, h, j: (q_row(b, h, j), h)),
        out_shape=jax.ShapeDtypeStruct((out_rows, heads * out_w), BF16),
        scratch_shapes=scratch,
        input_output_aliases=aliases,
        compiler_params=_params(3),
        name=name,
    )(*args)


def _drop_arg_kernel(kern, idx, *refs):
    kern(*(refs[:idx] + refs[idx + 1:]))


def _na_slab_start(rb, rows):
    return int(np.clip(rb * NA_ROWS_PER_BLOCK - NA_WIN_H // 2, 0, rows - NA_SLAB_ROWS))


def _na_kernel(rpb_ref, q_ref, k_ref, v_ref, kc_ref, vc_ref, o_ref, bias_ref, *, rows, scale):
    h = pl.program_id(0)
    n_blocks = rows // NA_ROWS_PER_BLOCK
    bw = 2 * NA_WIN_W - 1

    @pl.when(pl.program_id(1) == 0)
    def _build_bias():
        qc = lax.broadcasted_iota(jnp.int32, (GRID_W, GRID_W), 0)
        kc = lax.broadcasted_iota(jnp.int32, (GRID_W, GRID_W), 1)
        c0 = jnp.clip(qc - NA_WIN_W // 2, 0, GRID_W - NA_WIN_W)
        in_win = (kc >= c0) & (kc < c0 + NA_WIN_W)
        dc = jnp.clip(kc - qc, -(NA_WIN_W - 1), NA_WIN_W - 1) + (NA_WIN_W - 1)
        tiles = []
        for dr in range(2 * NA_WIN_H - 1):
            w = jnp.zeros((GRID_W, GRID_W), F32)
            for j in range(bw):
                w = jnp.where(dc == j, rpb_ref[h, dr * bw + j], w)
            tiles.append(jnp.where(in_win, w, NEG_INF))
        masked = jnp.full((GRID_W, GRID_W), NEG_INF, F32)
        for rb in range(n_blocks):
            u0 = _na_slab_start(rb, rows)
            for i in range(NA_ROWS_PER_BLOCK):
                r = rb * NA_ROWS_PER_BLOCK + i
                r0 = int(np.clip(r - NA_WIN_H // 2, 0, rows - NA_WIN_H))
                for u in range(NA_SLAB_ROWS):
                    kr = u0 + u
                    t = tiles[kr - r + NA_WIN_H - 1] if r0 <= kr < r0 + NA_WIN_H else masked
                    bias_ref[rb, i * GRID_W:(i + 1) * GRID_W, u * GRID_W:(u + 1) * GRID_W] = t

    c = float(np.log2(np.e))
    qb = NA_ROWS_PER_BLOCK * GRID_W
    for rb in range(n_blocks):
        k0 = _na_slab_start(rb, rows) * GRID_W
        k1 = k0 + NA_SLAB_ROWS * GRID_W
        q = q_ref[rb * qb:(rb + 1) * qb, :]
        s_win = _dot_t(q, k_ref[k0:k1, :]) * scale + bias_ref[rb]
        s_ctx = _dot_t(q, kc_ref[...]) * scale
        o = _softmax_pv([s_win, s_ctx], [v_ref[k0:k1, :], vc_ref[...]], c)
        o_ref[rb * qb:(rb + 1) * qb, :] = o.astype(o_ref.dtype)


def _na_attention(qkv, rpb, *, batch, seq, ctx, out_rows):
    n_lat = batch * seq
    rows = seq // GRID_W
    n_blocks = rows // NA_ROWS_PER_BLOCK
    cblk = n_lat // ctx
    H = NA_HEADS
    return pl.pallas_call(
        functools.partial(_na_kernel, rows=rows, scale=HEAD_DIM ** -0.5),
        grid=(H, batch),
        in_specs=[pl.BlockSpec(memory_space=pltpu.SMEM),
                  pl.BlockSpec((seq, HEAD_DIM), lambda h, b: (b, h)),
                  pl.BlockSpec((seq, HEAD_DIM), lambda h, b: (b, H + h)),
                  pl.BlockSpec((seq, HEAD_DIM), lambda h, b: (b, 2 * H + h)),
                  pl.BlockSpec((ctx, HEAD_DIM), lambda h, b: (cblk + b, H + h)),
                  pl.BlockSpec((ctx, HEAD_DIM), lambda h, b: (cblk + b, 2 * H + h))],
        out_specs=pl.BlockSpec((seq, HEAD_DIM), lambda h, b: (b, h)),
        out_shape=jax.ShapeDtypeStruct((out_rows, H * HEAD_DIM), BF16),
        scratch_shapes=[pltpu.VMEM((n_blocks, NA_ROWS_PER_BLOCK * GRID_W, NA_SLAB_ROWS * GRID_W), F32)],
        compiler_params=_params(2),
        name="na_attention",
    )(rpb.reshape(H, -1), qkv, qkv, qkv, qkv, qkv)


def _merge_kernel(h_ref, oa_ref, ob_ref, oc_ref, wga_ref, wgb_ref, wgc_ref,
                  wa_ref, wb_ref, wc_ref, y_ref):
    h = h_ref[...]
    y = (jax.nn.sigmoid(_dot(h, wga_ref[...])) * _dot(oa_ref[...], wa_ref[...])
         + jax.nn.sigmoid(_dot(h, wgb_ref[...])) * _dot(ob_ref[...], wb_ref[...])
         + jax.nn.sigmoid(_dot(h, wgc_ref[...])) * _dot(oc_ref[...], wc_ref[...]))
    y_ref[...] = y.astype(BF16)


def _merge(h, o_a, o_b, o_c, wg, wbr, m_rows, tm, tn):
    D = h.shape[1]
    W = o_a.shape[1]
    act = lambda k: pl.BlockSpec((tm, k), lambda i, n: (i, 0))
    wsp = lambda k: pl.BlockSpec((k, tn), lambda i, n: (0, n))
    return pl.pallas_call(
        _merge_kernel,
        grid=(m_rows // tm, D // tn),
        in_specs=[act(D), act(W), act(W), act(W), wsp(D), wsp(D), wsp(D), wsp(W), wsp(W), wsp(W)],
        out_specs=pl.BlockSpec((tm, tn), lambda i, n: (i, n)),
        out_shape=jax.ShapeDtypeStruct((m_rows, D), BF16),
        compiler_params=_params(2),
        name="merge",
    )(h, o_a, o_b, o_c, *wg, *wbr)


def _wo_kernel(y_ref, w_ref, x_ref, gpost_ref, gt_ref, gpre_ref, sc_ref, sh_ref, xo_ref, h_ref):
    z = _dot(y_ref[...], w_ref[...])
    xn = x_ref[...] + gt_ref[0] * _rms(z, gpost_ref[...])
    xo_ref[...] = xn
    h_ref[...] = (_rms(xn, gpre_ref[...]) * (1.0 + sc_ref[0]) + sh_ref[0]).astype(BF16)


def _wo(y, w_o, xc, g_post, g_pre2, rows, tm):
    M, D = y.shape
    row = lambda i: (i, 0)
    fixed = lambda i: (0, 0)
    return pl.pallas_call(
        _wo_kernel,
        grid=(M // tm,),
        in_specs=[pl.BlockSpec((tm, D), row),
                  pl.BlockSpec((D, D), fixed, pipeline_mode=pl.Buffered(1)),
                  pl.BlockSpec((tm, D), row),
                  pl.BlockSpec((1, D), fixed),
                  rows.spec(GT1),
                  pl.BlockSpec((1, D), fixed),
                  rows.spec(SC2), rows.spec(SH2)],
        out_specs=[pl.BlockSpec((tm, D), row), pl.BlockSpec((tm, D), row)],
        out_shape=[jax.ShapeDtypeStruct((M, D), F32), jax.ShapeDtypeStruct((M, D), BF16)],
        compiler_params=_params(1),
        name="out_proj",
    )(y, w_o, xc, g_post, rows.mod, g_pre2, rows.mod, rows.mod)


def _ffn_up_kernel(h_ref, w1_ref, w3_ref, u_ref):
    h = h_ref[...]
    u_ref[...] = (jax.nn.silu(_dot(h, w1_ref[...])) * _dot(h, w3_ref[...])).astype(BF16)


def _ffn_up(h2, w1, w3, tm, tn):
    M, D = h2.shape
    N = w1.shape[1]
    wsp = pl.BlockSpec((D, tn), lambda i, n: (0, n))
    return pl.pallas_call(
        _ffn_up_kernel,
        grid=(M // tm, N // tn),
        in_specs=[pl.BlockSpec((tm, D), lambda i, n: (i, 0)), wsp, wsp],
        out_specs=pl.BlockSpec((tm, tn), lambda i, n: (i, n)),
        out_shape=jax.ShapeDtypeStruct((M, N), BF16),
        compiler_params=_params(2),
        name="ffn_up",
    )(h2, w1, w3)


def _ffn_down_kernel(u_ref, w_ref, x_ref, gpost_ref, gt_ref, *rest, n_n, emit_h):
    if emit_h:
        gpre_ref, sc_ref, sh_ref, xo_ref, h_ref, z_ref = rest
    else:
        xo_ref, z_ref = rest
    n = pl.program_id(1)
    z_ref[n] = _dot(u_ref[...], w_ref[...])

    @pl.when(n == n_n - 1)
    def _():
        z = jnp.concatenate([z_ref[j] for j in range(n_n)], axis=-1)
        xn = x_ref[...] + gt_ref[0] * _rms(z, gpost_ref[...])
        xo_ref[...] = xn
        if emit_h:
            h_ref[...] = (_rms(xn, gpre_ref[...]) * (1.0 + sc_ref[0]) + sh_ref[0]).astype(BF16)


def _ffn_down(u, w2, xc, g_post, rows, tm, tn, next_pre=None):
    M, K = u.shape
    D = w2.shape[1]
    n_n = D // tn
    row = lambda i, n: (i, 0)
    fixed = lambda i, n: (0, 0)
    in_specs = [pl.BlockSpec((tm, K), row),
                pl.BlockSpec((K, tn), lambda i, n: (0, n)),
                pl.BlockSpec((tm, D), row),
                pl.BlockSpec((1, D), fixed),
                rows.spec(GT2)]
    args = [u, w2, xc, g_post, rows.mod]
    out_specs = [pl.BlockSpec((tm, D), row)]
    out_shape = [jax.ShapeDtypeStruct((M, D), F32)]
    if next_pre is not None:
        g_pre, nrows = next_pre
        in_specs += [pl.BlockSpec((1, D), fixed), nrows.spec(SC1), nrows.spec(SH1)]
        args += [g_pre, nrows.mod, nrows.mod]
        out_specs.append(pl.BlockSpec((tm, D), row))
        out_shape.append(jax.ShapeDtypeStruct((M, D), BF16))
    return pl.pallas_call(
        functools.partial(_ffn_down_kernel, n_n=n_n, emit_h=next_pre is not None),
        grid=(M // tm, n_n),
        in_specs=in_specs,
        out_specs=out_specs,
        out_shape=out_shape,
        scratch_shapes=[pltpu.VMEM((n_n, tm, tn), F32)],
        compiler_params=_params(2),
        name="ffn_down",
    )(*args)


def _rope_cols(w):
    x1r, x2r, x1c, x2c = jnp.split(w, 4, axis=-1)
    z = jnp.zeros(w.shape[:-1] + (LANES // 2 - 2 * x1r.shape[-1],), w.dtype)
    return jnp.concatenate([x1r, x1c, z, x2r, x2c, z], axis=-1)


def _rope_tables(seq, pad_rows, rot_dim):
    t = jnp.arange(seq)
    half = rot_dim // 2
    freqs = ROPE_THETA ** (-jnp.arange(0, half, 2, dtype=F32) / half)
    ang = [pos.astype(F32)[:, None] * freqs[None, :] for pos in (t // GRID_W, t % GRID_W)]
    cos = jnp.concatenate([jnp.cos(a) for a in ang], axis=-1)
    sin = jnp.concatenate([jnp.sin(a) for a in ang], axis=-1)
    pad = ((0, 0), (0, LANES // 2 - half))
    cos, sin = jnp.pad(cos, pad), jnp.pad(sin, pad)

    def finish(tab, ident):
        return jnp.concatenate([tab, jnp.full((pad_rows, LANES), ident, F32)], axis=0)
    return (finish(jnp.concatenate([cos, cos], axis=-1), 1.0),
            finish(jnp.concatenate([-sin, sin], axis=-1), 0.0))


def _cast_kernel(x_ref, o_ref):
    o_ref[...] = x_ref[...].astype(BF16)


def _to_bf16(w):
    w2 = w.reshape(-1, w.shape[-1])
    R, N = w2.shape
    tr = 256 if N > 4096 else 512
    out = pl.pallas_call(
        _cast_kernel,
        grid=(R // tr,),
        in_specs=[pl.BlockSpec((tr, N), lambda i: (i, 0))],
        out_specs=pl.BlockSpec((tr, N), lambda i: (i, 0)),
        out_shape=jax.ShapeDtypeStruct((R, N), BF16),
        compiler_params=_params(1),
        name="cast_bf16",
    )(w2)
    return out.reshape(w.shape)


def kernel(x, c, ctx, c_ctx, w_ada, b_ada, g_pre1, g_post1, g_pre2, g_post2, w_in, rpb,
           gqa_q_norm, gqa_k_norm, mla_q_norm, mla_kv_norm, w_uq, w_ukv,
           w_br_a, w_br_b, w_br_c, w_o, w_ff1, w_ff3, w_ff2):
    B, S, D = x.shape
    C = ctx.shape[1]
    L = w_ada.shape[0]
    n_lat, n_ctx = B * S, B * C
    T = n_lat + n_ctx
    assert B < MOD_ROWS and S // GRID_W == 32 and C % 256 == 0

    A_W = NA_HEADS * HEAD_DIM
    BQ, BKV = GQA_HEADS * HEAD_DIM, GQA_KV_HEADS * HEAD_DIM
    o_bq = 3 * A_W
    o_bk, o_bv = o_bq + BQ, o_bq + BQ + BKV
    o_cq = o_bv + BKV
    o_ckv, o_ckr = o_cq + MLA_Q_RANK, o_cq + MLA_Q_RANK + MLA_KV_RANK
    o_g = o_ckr + MLA_ROPE

    cc = jnp.zeros((MOD_ROWS, D), F32).at[:B].set(c).at[B].set(c_ctx)
    mod = _modulation(cc, w_ada, b_ada).reshape(L * MOD_ROWS * 6, 1, D)

    TM = next(t for t in (1024, 512, 256) if n_ctx % t == 0)
    TM_RES = min(TM, 512)
    gqa_tabs = _rope_tables(S, TM, HEAD_DIM)
    mla_tabs = _rope_tables(S, TM, MLA_ROPE)
    row2 = lambda v: v.reshape(1, -1)

    w_bra16, w_brb16, w_brc16, w_o16, w_ff116, w_ff316, w_ff216 = [
        _to_bf16(w) for w in (w_br_a, w_br_b, w_br_c, w_o, w_ff1, w_ff3, w_ff2)]
    n_qk = GQA_HEADS + GQA_KV_HEADS

    xc = h = None
    for l in range(L):
        last = l == L - 1
        wl = w_in[l]
        w_a = jnp.concatenate([wl[:, :o_bq], wl[:, o_bv:o_cq]], axis=1).astype(BF16)
        w_b = _rope_cols(wl[:, o_bq:o_bv].reshape(D, n_qk, HEAD_DIM)).reshape(D, -1).astype(BF16)
        w_c = jnp.concatenate([wl[:, o_cq:o_ckr], _rope_cols(wl[:, o_ckr:o_g])], axis=1).astype(BF16)
        w_g = [wl[:, o_g + j * D:o_g + (j + 1) * D].astype(BF16) for j in range(3)]
        wq = w_uq[l].reshape(MLA_Q_RANK, MLA_HEADS, MLA_NOPE + MLA_ROPE)
        wq = jnp.concatenate([wq[:, :, :MLA_NOPE], _rope_cols(wq[:, :, MLA_NOPE:])], axis=-1)
        wq = wq.reshape(MLA_Q_RANK, MLA_HEADS * 2 * LANES).astype(BF16)
        wkv = w_ukv[l].reshape(MLA_KV_RANK, MLA_HEADS, MLA_NOPE + MLA_V)
        wkv = jnp.concatenate([wkv[:, :, :MLA_NOPE].reshape(MLA_KV_RANK, -1),
                               wkv[:, :, MLA_NOPE:].reshape(MLA_KV_RANK, -1)], axis=1).astype(BF16)
        wbr = [w_bra16[l], w_brb16[l], w_brc16[l]]
        wo = w_o16[l]
        w1, w3, w2 = w_ff116[l], w_ff316[l], w_ff216[l]
        gains_b = jnp.concatenate([jnp.tile(_rope_cols(gqa_q_norm[l]), GQA_HEADS),
                                   jnp.tile(_rope_cols(gqa_k_norm[l]), GQA_KV_HEADS)]).reshape(1, -1)

        rows_res = _Rows(mod, l, TM_RES, n_lat, S, B)
        if l == 0:
            xc, h = _prenorm(x.reshape(n_lat, D), ctx.reshape(n_ctx, D), row2(g_pre1[l]),
                             rows_res, TM_RES)

        qkv = _mm_plain(h, w_a, TM, w_a.shape[1] // 2, "inproj_plain")
        qk_b = _mm_gqa_qk(h, w_b, gains_b, gqa_tabs, TM, w_b.shape[1] // 2, n_lat, S)
        cq, ckv, k_rope = _mm_mla_c(h, w_c, row2(mla_q_norm[l]), row2(mla_kv_norm[l]),
                                    mla_tabs, TM, n_lat, S)
        q_c = _mm_mla_q(cq, wq, mla_tabs, TM, wq.shape[1] // 2, n_lat, S)
        kv_c = _mm_plain(ckv, wkv, TM, wkv.shape[1] // 2, "mla_kv_up")

        H = NA_HEADS
        scale_b = HEAD_DIM ** -0.5
        scale_c = (MLA_NOPE + MLA_ROPE) ** -0.5
        m_rows = n_lat if last else T
        common = dict(batch=B, seq=S, ctx=C, out_rows=m_rows)
        grp = GQA_HEADS // GQA_KV_HEADS
        o_a = _na_attention(qkv, rpb[l], batch=B, seq=S, ctx=C, out_rows=m_rows)
        gqa = dict(q_arr=qk_b, q_col=lambda h: h, q_width=grp * HEAD_DIM,
                   k_arr=qk_b, k_col=lambda h: GQA_HEADS + h,
                   v_arr=qkv, v_col=lambda h: 3 * H + h, kr_arr=None,
                   groups=grp, heads=GQA_KV_HEADS, scale=scale_b, tq=512, sub=512, **common)
        mla = dict(q_arr=q_c, q_col=lambda h: h, q_width=2 * LANES,
                   k_arr=kv_c, k_col=lambda h: h, v_arr=kv_c, v_col=lambda h: MLA_HEADS + h,
                   kr_arr=k_rope, groups=1, heads=MLA_HEADS, scale=scale_c, tq=1024, sub=256,
                   **common)
        o_b = _attention(**gqa, ctx_queries=False, name="gqa_attention")
        o_c = _attention(**mla, ctx_queries=False, name="mla_attention")
        if not last:
            na_c = dict(q_arr=qkv, q_col=lambda h: h, q_width=HEAD_DIM,
                        k_arr=qkv, k_col=lambda h: H + h, v_arr=qkv, v_col=lambda h: 2 * H + h,
                        kr_arr=None, groups=1, heads=H, scale=scale_b, tq=C, sub=C, **common)
            o_a = _attention(**na_c, ctx_queries=True, out_prev=o_a, name="na_attention_ctx")
            o_b = _attention(**gqa, ctx_queries=True, out_prev=o_b, name="gqa_attention_ctx")
            o_c = _attention(**mla, ctx_queries=True, out_prev=o_c, name="mla_attention_ctx")

        y = _merge(h, o_a, o_b, o_c, w_g, wbr, m_rows, TM_RES, 512)
        xc, h2 = _wo(y, wo, xc, row2(g_post1[l]), row2(g_pre2[l]), rows_res, TM_RES)
        u = _ffn_up(h2, w1, w3, TM, 512)
        if last:
            xc, = _ffn_down(u, w2, xc, row2(g_post2[l]), rows_res, TM_RES, 512)
        else:
            nxt = (row2(g_pre1[l + 1]), _Rows(mod, l + 1, TM_RES, n_lat, S, B))
            xc, h = _ffn_down(u, w2, xc, row2(g_post2[l]), rows_res, TM_RES, 512, nxt)
    return xc.reshape(B, S, D)
```
